```python
import jax, jax.numpy as jnp
from jax import lax
import numpy as np

D_MODEL = 2048
BATCH = 4
SEQ = 4096
DEPTH = 1

CHUNK = 64
PLE_DIM = 256
D_CONV = D_MODEL
CONV_WIDTH = 3
RET_HEADS = 8
RET_DK = D_MODEL // RET_HEADS
RET_DV = 2 * RET_DK
RET_QK = RET_HEADS * RET_DK
RET_V = RET_HEADS * RET_DV
D_FF = -(-8 * D_MODEL // (3 * 256)) * 256
ROPE_BASE = 10000.0
EPS = 1e-6
SPLITS = (D_CONV, D_CONV, D_CONV, RET_QK, RET_QK, RET_V, RET_V, D_MODEL, D_MODEL)
N_IN = sum(SPLITS)

kernel_name = "hybrid_shortconv_retention_block"


def rms_norm(x, g):
    xf = x.astype(jnp.float32)
    y = xf * lax.rsqrt(jnp.mean(xf * xf, axis=-1, keepdims=True) + EPS)
    return (y * g.astype(jnp.float32)).astype(x.dtype)


def rotary(t, pos):
    half = t.shape[-1] // 2
    inv = ROPE_BASE ** (-jnp.arange(half, dtype=jnp.float32) / half)
    ang = pos.astype(jnp.float32)[:, None] * inv[None, :]
    cos = jnp.cos(ang)[None, :, None, :]
    sin = jnp.sin(ang)[None, :, None, :]
    tf = t.astype(jnp.float32)
    t1, t2 = tf[..., :half], tf[..., half:]
    return jnp.concatenate([t1 * cos - t2 * sin, t2 * cos + t1 * sin], axis=-1).astype(t.dtype)


def short_conv_mixer(b, c, v, conv_w):
    u = c * v
    s = u.shape[1]
    up = jnp.pad(u, ((0, 0), (CONV_WIDTH - 1, 0), (0, 0)))
    y = conv_w[0] * up[:, 0:s]
    for tap in range(1, CONV_WIDTH):
        y = y + conv_w[tap] * up[:, tap:tap + s]
    return b * y


def retention(q, k, v):
    bsz, s = q.shape[0], q.shape[1]
    n = s // CHUNK
    log_gamma = jnp.log1p(-jnp.exp2(-5.0 - jnp.arange(RET_HEADS, dtype=jnp.float32)))
    idx = jnp.arange(CHUNK, dtype=jnp.float32)
    d_intra = jnp.exp(log_gamma[:, None, None] * jnp.abs(idx[:, None] - idx[None, :]))
    xi = jnp.exp(log_gamma[:, None] * (idx + 1.0))
    zeta = jnp.exp(log_gamma[:, None] * (CHUNK - 1.0 - idx))
    g_chunk = jnp.exp(log_gamma * CHUNK)

    def to_chunks(t):
        return t.reshape(bsz, n, CHUNK, t.shape[2], t.shape[3]).transpose(0, 3, 1, 2, 4)

    qc, kc, vc = to_chunks(q), to_chunks(k), to_chunks(v)
    scores = jnp.einsum('bhncd,bhnmd->bhncm', qc, kc) * d_intra[None, :, None]
    o_intra = jnp.einsum('bhncm,bhnme->bhnce', scores, vc)

    def step(state, inp):
        q_i, k_i, v_i = inp
        o = jnp.einsum('bhcd,bhde->bhce', q_i, state) * xi[None, :, :, None]
        state = state * g_chunk[None, :, None, None] + jnp.einsum(
            'bhcd,bhce->bhde', k_i * zeta[None, :, :, None], v_i)
        return state, o

    xs = (qc.transpose(2, 0, 1, 3, 4), kc.transpose(2, 0, 1, 3, 4), vc.transpose(2, 0, 1, 3, 4))
    state0 = jnp.zeros((bsz, RET_HEADS, q.shape[-1], v.shape[-1]), jnp.float32)
    _, o_cross = lax.scan(step, state0, xs)
    o = o_intra + o_cross.transpose(1, 2, 0, 3, 4)
    return o.transpose(0, 2, 3, 1, 4).reshape(bsz, s, RET_HEADS, v.shape[-1])


def head_group_norm(o, g, dtype):
    of = o.astype(jnp.float32)
    mu = jnp.mean(of, axis=-1, keepdims=True)
    var = jnp.mean(jnp.square(of - mu), axis=-1, keepdims=True)
    y = (of - mu) * lax.rsqrt(var + EPS)
    y = y.reshape(o.shape[0], o.shape[1], RET_V) * g.astype(jnp.float32)
    return y.astype(dtype)


def setup_inputs(seed: int = 0) -> dict:
    key = jax.random.key(seed)
    ks = jax.random.split(key, 18)
    f32 = jnp.float32

    def w(k, shape, fan_in):
        return jax.random.normal(k, shape, f32) * (fan_in ** -0.5)

    def gain(k, shape):
        return 1.0 + 0.01 * jax.random.normal(k, shape, f32)

    return {
        "x": jax.random.normal(ks[0], (BATCH, SEQ, D_MODEL), f32),
        "p": jax.random.normal(ks[1], (DEPTH, BATCH, SEQ, PLE_DIM), f32),
        "g_mix": gain(ks[2], (DEPTH, D_MODEL)),
        "w_in": w(ks[3], (DEPTH, D_MODEL, N_IN), D_MODEL),
        "conv_w": w(ks[4], (DEPTH, CONV_WIDTH, D_CONV), CONV_WIDTH),
        "w_conv_out": w(ks[5], (DEPTH, D_CONV, D_MODEL), D_CONV),
        "g_ret": gain(ks[6], (DEPTH, RET_V)),
        "w_ret_out": w(ks[7], (DEPTH, RET_V, D_MODEL), RET_V),
        "w_o": w(ks[8], (DEPTH, D_MODEL, D_MODEL), D_MODEL),
        "g_ffn": gain(ks[9], (DEPTH, D_MODEL)),
        "w_ffn_in": w(ks[10], (DEPTH, D_MODEL, 2 * D_FF), D_MODEL),
        "w_ffn_out": w(ks[11], (DEPTH, D_FF, D_MODEL), D_FF),
        "g_ple": gain(ks[12], (DEPTH, D_MODEL)),
        "w_ple_gate": w(ks[13], (DEPTH, D_MODEL, D_MODEL), D_MODEL),
        "w_ple_proj": w(ks[14], (DEPTH, PLE_DIM, D_MODEL), PLE_DIM),
        "g_final": gain(ks[15], (D_MODEL,)),
    }


def reference(x, p, g_mix, w_in, conv_w, w_conv_out, g_ret, w_ret_out, w_o,
              g_ffn, w_ffn_in, w_ffn_out, g_ple, w_ple_gate, w_ple_proj, g_final):
    bsz, s, _ = x.shape
    pos = jnp.arange(s, dtype=jnp.int32)
    split_points = np.cumsum(SPLITS)[:-1].tolist()
    for i in range(DEPTH):
        h = rms_norm(x, g_mix[i])
        proj = h @ w_in[i]
        cb, cc, cv, q, k, v, g, gate_conv, gate_ret = jnp.split(proj, split_points, axis=-1)

        y_conv = short_conv_mixer(cb, cc, cv, conv_w[i]) @ w_conv_out[i]

        q = rotary(q.reshape(bsz, s, RET_HEADS, RET_DK), pos)
        k = rotary(k.reshape(bsz, s, RET_HEADS, RET_DK), pos) * (RET_DK ** -0.5)
        o = retention(q, k, v.reshape(bsz, s, RET_HEADS, RET_DV))
        o = head_group_norm(o, g_ret[i], x.dtype)
        y_ret = (jax.nn.silu(g) * o) @ w_ret_out[i]

        merged = jax.nn.sigmoid(gate_conv) * y_conv + jax.nn.sigmoid(gate_ret) * y_ret
        x = x + merged @ w_o[i]

        h = rms_norm(x, g_ffn[i])
        a, b = jnp.split(h @ w_ffn_in[i], 2, axis=-1)
        x = x + (jax.nn.silu(a) * b) @ w_ffn_out[i]

        ple_gate = jax.nn.sigmoid(rms_norm(x, g_ple[i]) @ w_ple_gate[i])
        x = x + ple_gate * (p[i] @ w_ple_proj[i])
    return rms_norm(x, g_final)
```

```python
import functools

import jax
import jax.numpy as jnp
import numpy as np
from jax import lax
from jax.experimental import pallas as pl
from jax.experimental.pallas import tpu as pltpu

RET_HEADS = 8
CHUNK = 64
CONV_WIDTH = 3
ROPE_BASE = 10000.0
EPS = 1e-6

RET_BLOCK = 256
VMEM_LIMIT_BYTES = 56 * 1024 * 1024
BF16_SUBLANES = 16

F32 = jnp.float32
BF16 = jnp.bfloat16


def _params(*sem):
    return pltpu.CompilerParams(dimension_semantics=sem, vmem_limit_bytes=VMEM_LIMIT_BYTES)


def _resident(shape, index_map):
    return pl.BlockSpec(shape, index_map, pipeline_mode=pl.Buffered(1))


def _rms(xf, g):
    return xf * lax.rsqrt(jnp.mean(xf * xf, axis=-1, keepdims=True) + EPS) * g


def _dot(a, b):
    return jnp.dot(a, b, preferred_element_type=F32)


def _in_proj_kernel(x_ref, g_ref, w_ref, cos_ref, sin_ref, o_ref, h_ref, *, groups, dk):
    j = pl.program_id(1)

    @pl.when(j == 0)
    def _():
        h_ref[...] = _rms(x_ref[...], g_ref[...]).astype(BF16)

    acc = _dot(h_ref[...], w_ref[...])
    (q_lo, q_hi), (k_lo, k_hi), (g_lo, g_hi), (s_lo, s_hi) = groups
    half = dk // 2

    def rotary(scale):
        cos = cos_ref[...] * scale
        sin = sin_ref[...] * scale
        for c in range(0, acc.shape[1], dk):
            t1 = acc[:, c:c + half]
            t2 = acc[:, c + half:c + dk]
            o_ref[:, c:c + half] = (t1 * cos - t2 * sin).astype(BF16)
            o_ref[:, c + half:c + dk] = (t2 * cos + t1 * sin).astype(BF16)

    @pl.when((j >= q_lo) & (j < q_hi))
    def _():
        rotary(1.0)

    @pl.when((j >= k_lo) & (j < k_hi))
    def _():
        rotary(float(dk) ** -0.5)

    @pl.when((j >= g_lo) & (j < g_hi))
    def _():
        o_ref[...] = (acc * jax.nn.sigmoid(acc)).astype(BF16)

    @pl.when((j >= s_lo) & (j < s_hi))
    def _():
        o_ref[...] = jax.nn.sigmoid(acc).astype(BF16)

    plain = ((j < q_lo) | ((j >= k_hi) & (j < g_lo)))

    @pl.when(plain)
    def _():
        o_ref[...] = acc.astype(BF16)


def _in_proj(x2d, g_mix, w_in, cos, sin, *, seq, d, tm, tn):
    t, n_in = x2d.shape[0], w_in.shape[1]
    dk = d // RET_HEADS
    b = lambda cols: cols // tn
    groups = ((b(3 * d), b(4 * d)), (b(4 * d), b(5 * d)), (b(7 * d), b(9 * d)), (b(9 * d), b(11 * d)))
    pos_blocks = seq // tm
    return pl.pallas_call(
        functools.partial(_in_proj_kernel, groups=groups, dk=dk),
        grid=(t // tm, n_in // tn),
        in_specs=[
            pl.BlockSpec((tm, d), lambda i, j: (i, 0)),
            pl.BlockSpec((1, d), lambda i, j: (0, 0)),
            pl.BlockSpec((d, tn), lambda i, j: (0, j)),
            pl.BlockSpec((tm, dk // 2), lambda i, j: (i % pos_blocks, 0)),
            pl.BlockSpec((tm, dk // 2), lambda i, j: (i % pos_blocks, 0)),
        ],
        out_specs=pl.BlockSpec((tm, tn), lambda i, j: (i, j)),
        out_shape=jax.ShapeDtypeStruct((t, n_in), BF16),
        scratch_shapes=[pltpu.VMEM((tm, d), BF16)],
        compiler_params=_params("parallel", "arbitrary"),
        name="in_proj",
    )(x2d, g_mix, w_in, cos, sin)


def _retention_kernel(q_ref, k_ref, v_ref, sg_ref, dm_ref, rd_ref, cd_ref, gl_ref, gr_ref,
                      o_ref, state_ref):
    @pl.when(pl.program_id(2) == 0)
    def _():
        state_ref[...] = jnp.zeros_like(state_ref)

    q = q_ref[...]
    k = k_ref[...]
    v = v_ref[...]
    state = state_ref[...]
    s = lax.dot_general(q, k, (((1,), (1,)), ((), ())), preferred_element_type=F32)
    p = (s * dm_ref[0]).astype(BF16)
    o = _dot(p, v) + _dot(q, state.astype(BF16)) * rd_ref[0]
    kd = (k.astype(F32) * cd_ref[0]).astype(BF16)
    state_ref[...] = state * gl_ref[0] + lax.dot_general(
        kd, v, (((0,), (0,)), ((), ())), preferred_element_type=F32)

    mu = jnp.mean(o, axis=-1, keepdims=True)
    oc = o - mu
    var = jnp.mean(oc * oc, axis=-1, keepdims=True)
    y = oc * lax.rsqrt(var + EPS) * gr_ref[...]
    o_ref[...] = (sg_ref[...].astype(F32) * y).astype(BF16)


def _retention_tables(dk, dv):
    lg = jnp.log1p(-jnp.exp2(-5.0 - jnp.arange(RET_HEADS, dtype=F32)))[:, None, None]
    r = jnp.arange(RET_BLOCK, dtype=F32)
    diff = r[:, None] - r[None, :]
    cr = (jnp.arange(RET_BLOCK) // CHUNK)
    same = cr[:, None] == cr[None, :]
    past = cr[:, None] > cr[None, :]
    dist = jnp.where(same, jnp.abs(diff), diff)
    dm = jnp.where(same | past, jnp.exp(lg * dist[None]), 0.0)
    rd = jnp.broadcast_to(jnp.exp(lg * (r + 1.0)[None, :, None]), (RET_HEADS, RET_BLOCK, dv))
    cd = jnp.broadcast_to(jnp.exp(lg * (RET_BLOCK - 1.0 - r)[None, :, None]), (RET_HEADS, RET_BLOCK, dk))
    gl = jnp.broadcast_to(jnp.exp(lg * float(RET_BLOCK)), (RET_HEADS, 1, dv))
    return dm, rd, cd, gl


def _retention(proj, g_ret, *, bsz, seq, d):
    t = proj.shape[0]
    dk = d // RET_HEADS
    dv = 2 * dk
    lb = RET_BLOCK
    nb = seq // lb
    dm, rd, cd, gl = _retention_tables(dk, dv)
    row = lambda b, h, n: b * nb + n
    q0, k0, v0, g0 = 3 * d // dk, 4 * d // dk, 5 * d // dv, 7 * d // dv
    return pl.pallas_call(
        _retention_kernel,
        grid=(bsz, RET_HEADS, nb),
        in_specs=[
            pl.BlockSpec((lb, dk), lambda b, h, n: (row(b, h, n), q0 + h)),
            pl.BlockSpec((lb, dk), lambda b, h, n: (row(b, h, n), k0 + h)),
            pl.BlockSpec((lb, dv), lambda b, h, n: (row(b, h, n), v0 + h)),
            pl.BlockSpec((lb, dv), lambda b, h, n: (row(b, h, n), g0 + h)),
            pl.BlockSpec((1, lb, lb), lambda b, h, n: (h, 0, 0)),
            pl.BlockSpec((1, lb, dv), lambda b, h, n: (h, 0, 0)),
            pl.BlockSpec((1, lb, dk), lambda b, h, n: (h, 0, 0)),
            pl.BlockSpec((1, 1, dv), lambda b, h, n: (h, 0, 0)),
            pl.BlockSpec((1, dv), lambda b, h, n: (0, h)),
        ],
        out_specs=pl.BlockSpec((lb, dv), lambda b, h, n: (row(b, h, n), h)),
        out_shape=jax.ShapeDtypeStruct((t, RET_HEADS * dv), BF16),
        scratch_shapes=[pltpu.VMEM((dk, dv), F32)],
        compiler_params=_params("parallel", "parallel", "arbitrary"),
        name="retention",
    )(proj, proj, proj, proj, dm, rd, cd, gl, g_ret)


def _mix_kernel(cb_ref, cc_ref, cv_ref, hc_ref, hv_ref, sgc_ref, sgr_ref, ar_ref, cw_ref,
                wc_ref, wr_ref, o_ref, *, seq):
    tm = cb_ref.shape[0]
    halo = hc_ref.shape[0]
    u = cc_ref[...].astype(F32) * cv_ref[...].astype(F32)
    uh = hc_ref[...].astype(F32) * hv_ref[...].astype(F32)
    at_seq_start = (pl.program_id(0) * tm) % seq == 0
    uh = jnp.where(at_seq_start, 0.0, uh)
    ue = jnp.concatenate([uh, u], axis=0)
    u1 = pltpu.roll(ue, 1, 0)[halo:]
    u2 = pltpu.roll(ue, 2, 0)[halo:]
    cw = cw_ref[...]
    y = cw[0:1] * u2 + cw[1:2] * u1 + cw[2:3] * u
    a_conv = (cb_ref[...].astype(F32) * y).astype(BF16)
    y_conv = _dot(a_conv, wc_ref[...])
    y_ret = _dot(ar_ref[...], wr_ref[...])
    merged = sgc_ref[...].astype(F32) * y_conv + sgr_ref[...].astype(F32) * y_ret
    o_ref[...] = merged.astype(BF16)


def _mix(proj, a_ret, conv_w, w_conv_out, w_ret_out, *, seq, d, tm):
    t = proj.shape[0]
    halo = BF16_SUBLANES
    hb = tm // halo
    prev = lambda i: jnp.maximum(i * hb - 1, 0)
    return pl.pallas_call(
        functools.partial(_mix_kernel, seq=seq),
        grid=(t // tm,),
        in_specs=[
            pl.BlockSpec((tm, d), lambda i: (i, 0)),
            pl.BlockSpec((tm, d), lambda i: (i, 1)),
            pl.BlockSpec((tm, d), lambda i: (i, 2)),
            pl.BlockSpec((halo, d), lambda i: (prev(i), 1)),
            pl.BlockSpec((halo, d), lambda i: (prev(i), 2)),
            pl.BlockSpec((tm, d), lambda i: (i, 9)),
            pl.BlockSpec((tm, d), lambda i: (i, 10)),
            pl.BlockSpec((tm, 2 * d), lambda i: (i, 0)),
            pl.BlockSpec((CONV_WIDTH, d), lambda i: (0, 0)),
            _resident((d, d), lambda i: (0, 0)),
            _resident((2 * d, d), lambda i: (0, 0)),
        ],
        out_specs=pl.BlockSpec((tm, d), lambda i: (i, 0)),
        out_shape=jax.ShapeDtypeStruct((t, d), BF16),
        compiler_params=_params("parallel"),
        name="mix",
    )(proj, proj, proj, proj, proj, proj, proj, a_ret, conv_w, w_conv_out, w_ret_out)


def _out_proj_kernel(m_ref, x_ref, w_ref, g_ref, x1_ref, h_ref):
    x1 = x_ref[...] + _dot(m_ref[...], w_ref[...])
    x1_ref[...] = x1
    h_ref[...] = _rms(x1, g_ref[...]).astype(BF16)


def _out_proj(merged, x2d, w_o, g_ffn, *, d, tm):
    t = x2d.shape[0]
    return pl.pallas_call(
        _out_proj_kernel,
        grid=(t // tm,),
        in_specs=[
            pl.BlockSpec((tm, d), lambda i: (i, 0)),
            pl.BlockSpec((tm, d), lambda i: (i, 0)),
            _resident((d, d), lambda i: (0, 0)),
            pl.BlockSpec((1, d), lambda i: (0, 0)),
        ],
        out_specs=[pl.BlockSpec((tm, d), lambda i: (i, 0)), pl.BlockSpec((tm, d), lambda i: (i, 0))],
        out_shape=[jax.ShapeDtypeStruct((t, d), F32), jax.ShapeDtypeStruct((t, d), BF16)],
        compiler_params=_params("parallel"),
        name="out_proj",
    )(merged, x2d, w_o, g_ffn)


def _ffn_kernel(h_ref, x_ref, wa_ref, wb_ref, wo_ref, g_ref, x2_ref, h3_ref, acc_ref):
    f = pl.program_id(1)

    @pl.when(f == 0)
    def _():
        acc_ref[...] = x_ref[...]

    h = h_ref[...]
    a = _dot(h, wa_ref[...])
    b = _dot(h, wb_ref[...])
    act = (a * jax.nn.sigmoid(a) * b).astype(BF16)
    acc_ref[...] += _dot(act, wo_ref[...])

    @pl.when(f == pl.num_programs(1) - 1)
    def _():
        x2 = acc_ref[...]
        x2_ref[...] = x2
        h3_ref[...] = _rms(x2, g_ref[...]).astype(BF16)


def _ffn(h2, x1, w_ffn_in, w_ffn_out, g_ple, *, d, tm, tf):
    t = x1.shape[0]
    d_ff = w_ffn_out.shape[0]
    nf = d_ff // tf
    return pl.pallas_call(
        _ffn_kernel,
        grid=(t // tm, nf),
        in_specs=[
            pl.BlockSpec((tm, d), lambda i, f: (i, 0)),
            pl.BlockSpec((tm, d), lambda i, f: (i, 0)),
            pl.BlockSpec((d, tf), lambda i, f: (0, f)),
            pl.BlockSpec((d, tf), lambda i, f: (0, nf + f)),
            pl.BlockSpec((tf, d), lambda i, f: (f, 0)),
            pl.BlockSpec((1, d), lambda i, f: (0, 0)),
        ],
        out_specs=[pl.BlockSpec((tm, d), lambda i, f: (i, 0)), pl.BlockSpec((tm, d), lambda i, f: (i, 0))],
        out_shape=[jax.ShapeDtypeStruct((t, d), F32), jax.ShapeDtypeStruct((t, d), BF16)],
        scratch_shapes=[pltpu.VMEM((tm, d), F32)],
        compiler_params=_params("parallel", "arbitrary"),
        name="ffn",
    )(h2, x1, w_ffn_in, w_ffn_in, w_ffn_out, g_ple)


def _ple_kernel(h_ref, x_ref, p_ref, wg_ref, wp_ref, g_ref, o_ref):
    gate = jax.nn.sigmoid(_dot(h_ref[...], wg_ref[...]))
    pp = _dot(p_ref[...].astype(BF16), wp_ref[...])
    x3 = x_ref[...] + gate * pp
    o_ref[...] = _rms(x3, g_ref[...])


def _ple(h3, x2, p2d, w_gate, w_proj, g_final, *, d, tm):
    t = x2.shape[0]
    pd = p2d.shape[1]
    return pl.pallas_call(
        _ple_kernel,
        grid=(t // tm,),
        in_specs=[
            pl.BlockSpec((tm, d), lambda i: (i, 0)),
            pl.BlockSpec((tm, d), lambda i: (i, 0)),
            pl.BlockSpec((tm, pd), lambda i: (i, 0)),
            _resident((d, d), lambda i: (0, 0)),
            _resident((pd, d), lambda i: (0, 0)),
            pl.BlockSpec((1, d), lambda i: (0, 0)),
        ],
        out_specs=pl.BlockSpec((tm, d), lambda i: (i, 0)),
        out_shape=jax.ShapeDtypeStruct((t, d), F32),
        compiler_params=_params("parallel"),
        name="ple",
    )(h3, x2, p2d, w_gate, w_proj, g_final)


def _rotary_tables(seq, dk):
    half = dk // 2
    inv = ROPE_BASE ** (-jnp.arange(half, dtype=F32) / half)
    ang = jnp.arange(seq, dtype=jnp.int32).astype(F32)[:, None] * inv[None, :]
    return jnp.cos(ang), jnp.sin(ang)


def _tiles(t, seq, d, d_ff):
    pick = lambda n, want: max(c for c in range(1, want + 1) if n % c == 0)
    return dict(
        in_tm=pick(seq, 1024), in_tn=pick(d, 1024),
        mix_tm=pick(seq, 256), out_tm=pick(seq, 512),
        ffn_tm=pick(seq, 512), ffn_tf=pick(d_ff, 512), ple_tm=pick(seq, 512),
    )


def kernel(x, p, g_mix, w_in, conv_w, w_conv_out, g_ret, w_ret_out, w_o, g_ffn, w_ffn_in,
           w_ffn_out, g_ple, w_ple_gate, w_ple_proj, g_final):
    bsz, seq, d = x.shape
    depth = w_in.shape[0]
    t = bsz * seq
    d_ff = w_ffn_out.shape[1]
    dk = d // RET_HEADS
    assert seq % RET_BLOCK == 0 and RET_BLOCK % CHUNK == 0
    ts = _tiles(t, seq, d, d_ff)
    cos, sin = _rotary_tables(seq, dk)
    bf = lambda w: w.astype(BF16)

    assert depth == 1, depth
    xc = x.reshape(t, d)
    proj = _in_proj(xc, g_mix, bf(w_in[0]), cos, sin, seq=seq, d=d, tm=ts["in_tm"], tn=ts["in_tn"])
    a_ret = _retention(proj, g_ret, bsz=bsz, seq=seq, d=d)
    merged = _mix(proj, a_ret, conv_w[0], bf(w_conv_out[0]), bf(w_ret_out[0]),
                  seq=seq, d=d, tm=ts["mix_tm"])
    x1, h2 = _out_proj(merged, xc, bf(w_o[0]), g_ffn, d=d, tm=ts["out_tm"])
    x2, h3 = _ffn(h2, x1, bf(w_ffn_in[0]), bf(w_ffn_out[0]), g_ple,
                  d=d, tm=ts["ffn_tm"], tf=ts["ffn_tf"])
    out = _ple(h3, x2, p[0].reshape(t, -1), bf(w_ple_gate[0]), bf(w_ple_proj[0]), g_final[None],
               d=d, tm=ts["ple_tm"])
    return out.reshape(bsz, seq, d)
```

```python
import functools

import jax
import jax.numpy as jnp
from jax import lax
from jax.experimental import pallas as pl
from jax.experimental.pallas import tpu as pltpu

RET_HEADS = 8
CHUNK = 64
CONV_WIDTH = 3
ROPE_BASE = 10000.0
EPS = 1e-6

RET_BLOCK = 256
RET_ROWS_PER_STEP = 1024
VMEM_LIMIT_BYTES = 56 * 1024 * 1024
BF16_SUBLANES = 16

F32 = jnp.float32
BF16 = jnp.bfloat16


def _params(*sem):
    return pltpu.CompilerParams(dimension_semantics=sem, vmem_limit_bytes=VMEM_LIMIT_BYTES)


def _resident(shape, index_map):
    return pl.BlockSpec(shape, index_map, pipeline_mode=pl.Buffered(1))


def _rms(xf, g):
    return xf * lax.rsqrt(jnp.mean(xf * xf, axis=-1, keepdims=True) + EPS) * g


def _dot(a, b):
    return jnp.dot(a, b, preferred_element_type=F32)


def _proj_plain_kernel(x_ref, g_ref, w_ref, o_ref, h_ref):
    @pl.when(pl.program_id(1) == 0)
    def _():
        h_ref[...] = _rms(x_ref[...], g_ref[...]).astype(BF16)

    o_ref[...] = _dot(h_ref[...], w_ref[...]).astype(BF16)


def _proj_rotary_kernel(h_ref, w_ref, cos_ref, sin_ref, o_ref, *, dk):
    acc = _dot(h_ref[...], w_ref[...])
    cos = cos_ref[0]
    sin = sin_ref[0]
    half = dk // 2
    for c in range(0, acc.shape[1], dk):
        t1 = acc[:, c:c + half]
        t2 = acc[:, c + half:c + dk]
        o_ref[:, c:c + half] = (t1 * cos - t2 * sin).astype(BF16)
        o_ref[:, c + half:c + dk] = (t2 * cos + t1 * sin).astype(BF16)


def _proj_act_kernel(h_ref, w_ref, o_ref, *, silu_blocks):
    acc = _dot(h_ref[...], w_ref[...])
    is_silu = pl.program_id(1) < silu_blocks
    o_ref[...] = (jax.nn.sigmoid(acc) * jnp.where(is_silu, acc, 1.0)).astype(BF16)


def _in_proj(x2d, g_mix, w_in, cos, sin, *, seq, d, tm, tn):
    t = x2d.shape[0]
    dk = d // RET_HEADS
    nb = d // tn
    pos_blocks = seq // tm
    grid_m = t // tm
    sem = _params("parallel", "arbitrary")
    pv, h = pl.pallas_call(
        _proj_plain_kernel,
        grid=(grid_m, 5 * nb),
        in_specs=[
            pl.BlockSpec((tm, d), lambda i, j: (i, 0)),
            pl.BlockSpec((1, d), lambda i, j: (0, 0)),
            pl.BlockSpec((d, tn), lambda i, j: (0, jnp.where(j < 3 * nb, j, j + 2 * nb))),
        ],
        out_specs=[pl.BlockSpec((tm, tn), lambda i, j: (i, j)),
                   pl.BlockSpec((tm, d), lambda i, j: (i, 0))],
        out_shape=[jax.ShapeDtypeStruct((t, 5 * d), BF16), jax.ShapeDtypeStruct((t, d), BF16)],
        compiler_params=sem,
        name="proj_plain",
    )(x2d, g_mix, w_in)
    qk = pl.pallas_call(
        functools.partial(_proj_rotary_kernel, dk=dk),
        grid=(grid_m, 2 * nb),
        in_specs=[
            pl.BlockSpec((tm, d), lambda i, j: (i, 0)),
            pl.BlockSpec((d, tn), lambda i, j: (0, 3 * nb + j)),
            pl.BlockSpec((1, tm, dk // 2), lambda i, j: (j // nb, i % pos_blocks, 0)),
            pl.BlockSpec((1, tm, dk // 2), lambda i, j: (j // nb, i % pos_blocks, 0)),
        ],
        out_specs=pl.BlockSpec((tm, tn), lambda i, j: (i, j)),
        out_shape=jax.ShapeDtypeStruct((t, 2 * d), BF16),
        compiler_params=sem,
        name="proj_rotary",
    )(h, w_in, cos, sin)
    act = pl.pallas_call(
        functools.partial(_proj_act_kernel, silu_blocks=2 * nb),
        grid=(grid_m, 4 * nb),
        in_specs=[
            pl.BlockSpec((tm, d), lambda i, j: (i, 0)),
            pl.BlockSpec((d, tn), lambda i, j: (0, 7 * nb + j)),
        ],
        out_specs=pl.BlockSpec((tm, tn), lambda i, j: (i, j)),
        out_shape=jax.ShapeDtypeStruct((t, 4 * d), BF16),
        compiler_params=sem,
        name="proj_act",
    )(h, w_in)
    return pv, qk, act


def _rotary_tables(seq, dk):
    half = dk // 2
    inv = ROPE_BASE ** (-jnp.arange(half, dtype=F32) / half)
    ang = jnp.arange(seq, dtype=jnp.int32).astype(F32)[:, None] * inv[None, :]
    scale = jnp.array([1.0, float(dk) ** -0.5], F32)[:, None, None]
    return jnp.cos(ang)[None] * scale, jnp.sin(ang)[None] * scale


def _retention_kernel(q_ref, k_ref, v_ref, sg_ref, dm_ref, rd_ref, cd_ref, gl_ref, gr_ref,
                      o_ref, state_ref):
    @pl.when(pl.program_id(2) == 0)
    def _():
        state_ref[...] = jnp.zeros_like(state_ref)

    lb = dm_ref.shape[1]
    state = state_ref[...]
    for r0 in range(0, q_ref.shape[0], lb):
        rows = pl.ds(r0, lb)
        q = q_ref[rows, :]
        k = k_ref[rows, :]
        v = v_ref[rows, :]
        s = lax.dot_general(q, k, (((1,), (1,)), ((), ())), preferred_element_type=F32)
        p = (s * dm_ref[0]).astype(BF16)
        o = _dot(p, v) + _dot(q, state.astype(BF16)) * rd_ref[0]
        kd = (k.astype(F32) * cd_ref[0]).astype(BF16)
        state = state * gl_ref[0] + lax.dot_general(
            kd, v, (((0,), (0,)), ((), ())), preferred_element_type=F32)

        mu = jnp.mean(o, axis=-1, keepdims=True)
        oc = o - mu
        var = jnp.mean(oc * oc, axis=-1, keepdims=True)
        y = oc * lax.rsqrt(var + EPS) * gr_ref[...]
        o_ref[rows, :] = (sg_ref[rows, :].astype(F32) * y).astype(BF16)
    state_ref[...] = state


def _retention_tables(dk, dv):
    lg = jnp.log1p(-jnp.exp2(-5.0 - jnp.arange(RET_HEADS, dtype=F32)))[:, None, None]
    r = jnp.arange(RET_BLOCK, dtype=F32)
    diff = r[:, None] - r[None, :]
    cr = (jnp.arange(RET_BLOCK) // CHUNK)
    same = cr[:, None] == cr[None, :]
    past = cr[:, None] > cr[None, :]
    dist = jnp.where(same, jnp.abs(diff), diff)
    dm = jnp.where(same | past, jnp.exp(lg * dist[None]), 0.0)
    rd = jnp.broadcast_to(jnp.exp(lg * (r + 1.0)[None, :, None]), (RET_HEADS, RET_BLOCK, dv))
    cd = jnp.broadcast_to(jnp.exp(lg * (RET_BLOCK - 1.0 - r)[None, :, None]), (RET_HEADS, RET_BLOCK, dk))
    gl = jnp.broadcast_to(jnp.exp(lg * float(RET_BLOCK)), (RET_HEADS, 1, dv))
    return dm, rd, cd, gl


def _retention(qk, pv, act, g_ret, *, bsz, seq, d, rows):
    t = qk.shape[0]
    dk = d // RET_HEADS
    dv = 2 * dk
    lb = RET_BLOCK
    ns = seq // rows
    dm, rd, cd, gl = _retention_tables(dk, dv)
    row = lambda b, h, n: b * ns + n
    k0 = d // dk
    v0 = 3 * d // dv
    return pl.pallas_call(
        _retention_kernel,
        grid=(bsz, RET_HEADS, ns),
        in_specs=[
            pl.BlockSpec((rows, dk), lambda b, h, n: (row(b, h, n), h)),
            pl.BlockSpec((rows, dk), lambda b, h, n: (row(b, h, n), k0 + h)),
            pl.BlockSpec((rows, dv), lambda b, h, n: (row(b, h, n), v0 + h)),
            pl.BlockSpec((rows, dv), lambda b, h, n: (row(b, h, n), h)),
            pl.BlockSpec((1, lb, lb), lambda b, h, n: (h, 0, 0)),
            pl.BlockSpec((1, lb, dv), lambda b, h, n: (h, 0, 0)),
            pl.BlockSpec((1, lb, dk), lambda b, h, n: (h, 0, 0)),
            pl.BlockSpec((1, 1, dv), lambda b, h, n: (h, 0, 0)),
            pl.BlockSpec((1, dv), lambda b, h, n: (0, h)),
        ],
        out_specs=pl.BlockSpec((rows, dv), lambda b, h, n: (row(b, h, n), h)),
        out_shape=jax.ShapeDtypeStruct((t, RET_HEADS * dv), BF16),
        scratch_shapes=[pltpu.VMEM((dk, dv), F32)],
        compiler_params=_params("parallel", "parallel", "arbitrary"),
        name="retention",
    )(qk, qk, pv, act, dm, rd, cd, gl, g_ret)


def _mix_kernel(cb_ref, cc_ref, cv_ref, hc_ref, hv_ref, sgc_ref, sgr_ref, ar_ref, cw_ref,
                wc_ref, wr_ref, o_ref, *, seq):
    tm = cb_ref.shape[0]
    halo = hc_ref.shape[0]
    u = cc_ref[...].astype(F32) * cv_ref[...].astype(F32)
    uh = hc_ref[...].astype(F32) * hv_ref[...].astype(F32)
    at_seq_start = (pl.program_id(0) * tm) % seq == 0
    uh = jnp.where(at_seq_start, 0.0, uh)
    ue = jnp.concatenate([uh, u], axis=0)
    u1 = pltpu.roll(ue, 1, 0)[halo:]
    u2 = pltpu.roll(ue, 2, 0)[halo:]
    cw = cw_ref[...]
    y = cw[0:1] * u2 + cw[1:2] * u1 + cw[2:3] * u
    a_conv = (cb_ref[...].astype(F32) * y).astype(BF16)
    y_conv = _dot(a_conv, wc_ref[...])
    y_ret = _dot(ar_ref[...], wr_ref[...])
    merged = sgc_ref[...].astype(F32) * y_conv + sgr_ref[...].astype(F32) * y_ret
    o_ref[...] = merged.astype(BF16)


def _mix(pv, act, a_ret, conv_w, w_conv_out, w_ret_out, *, seq, d, tm):
    t = pv.shape[0]
    halo = BF16_SUBLANES
    hb = tm // halo
    prev = lambda i: jnp.maximum(i * hb - 1, 0)
    return pl.pallas_call(
        functools.partial(_mix_kernel, seq=seq),
        grid=(t // tm,),
        in_specs=[
            pl.BlockSpec((tm, d), lambda i: (i, 0)),
            pl.BlockSpec((tm, d), lambda i: (i, 1)),
            pl.BlockSpec((tm, d), lambda i: (i, 2)),
            pl.BlockSpec((halo, d), lambda i: (prev(i), 1)),
            pl.BlockSpec((halo, d), lambda i: (prev(i), 2)),
            pl.BlockSpec((tm, d), lambda i: (i, 2)),
            pl.BlockSpec((tm, d), lambda i: (i, 3)),
            pl.BlockSpec((tm, 2 * d), lambda i: (i, 0)),
            pl.BlockSpec((CONV_WIDTH, d), lambda i: (0, 0)),
            _resident((d, d), lambda i: (0, 0)),
            _resident((2 * d, d), lambda i: (0, 0)),
        ],
        out_specs=pl.BlockSpec((tm, d), lambda i: (i, 0)),
        out_shape=jax.ShapeDtypeStruct((t, d), BF16),
        compiler_params=_params("parallel"),
        name="mix",
    )(pv, pv, pv, pv, pv, act, act, a_ret, conv_w, w_conv_out, w_ret_out)


def _out_proj_kernel(m_ref, x_ref, w_ref, g_ref, x1_ref, h_ref):
    x1 = x_ref[...] + _dot(m_ref[...], w_ref[...])
    x1_ref[...] = x1
    h_ref[...] = _rms(x1, g_ref[...]).astype(BF16)


def _out_proj(merged, x2d, w_o, g_ffn, *, d, tm):
    t = x2d.shape[0]
    return pl.pallas_call(
        _out_proj_kernel,
        grid=(t // tm,),
        in_specs=[
            pl.BlockSpec((tm, d), lambda i: (i, 0)),
            pl.BlockSpec((tm, d), lambda i: (i, 0)),
            _resident((d, d), lambda i: (0, 0)),
            pl.BlockSpec((1, d), lambda i: (0, 0)),
        ],
        out_specs=[pl.BlockSpec((tm, d), lambda i: (i, 0)), pl.BlockSpec((tm, d), lambda i: (i, 0))],
        out_shape=[jax.ShapeDtypeStruct((t, d), F32), jax.ShapeDtypeStruct((t, d), BF16)],
        compiler_params=_params("parallel"),
        name="out_proj",
    )(merged, x2d, w_o, g_ffn)


def _ffn_kernel(h_ref, wa_ref, wb_ref, wo_ref, y_ref):
    @pl.when(pl.program_id(1) == 0)
    def _():
        y_ref[...] = jnp.zeros_like(y_ref)

    h = h_ref[...]
    a = _dot(h, wa_ref[...])
    b = _dot(h, wb_ref[...])
    act = (a * jax.nn.sigmoid(a) * b).astype(BF16)
    y_ref[...] += _dot(act, wo_ref[...])


def _ffn(h2, w_ffn_in, w_ffn_out, *, d, tm, tf):
    t = h2.shape[0]
    d_ff = w_ffn_out.shape[0]
    nf = d_ff // tf
    return pl.pallas_call(
        _ffn_kernel,
        grid=(t // tm, nf),
        in_specs=[
            pl.BlockSpec((tm, d), lambda i, f: (i, 0)),
            pl.BlockSpec((d, tf), lambda i, f: (0, f)),
            pl.BlockSpec((d, tf), lambda i, f: (0, nf + f)),
            pl.BlockSpec((tf, d), lambda i, f: (f, 0)),
        ],
        out_specs=pl.BlockSpec((tm, d), lambda i, f: (i, 0)),
        out_shape=jax.ShapeDtypeStruct((t, d), F32),
        compiler_params=_params("parallel", "arbitrary"),
        name="ffn",
    )(h2, w_ffn_in, w_ffn_in, w_ffn_out)


def _ple_kernel(x_ref, y_ref, p_ref, wg_ref, wp_ref, gp_ref, gf_ref, o_ref):
    x2 = x_ref[...] + y_ref[...]
    h3 = _rms(x2, gp_ref[...]).astype(BF16)
    gate = jax.nn.sigmoid(_dot(h3, wg_ref[...]))
    pp = _dot(p_ref[...].astype(BF16), wp_ref[...])
    x3 = x2 + gate * pp
    o_ref[...] = _rms(x3, gf_ref[...])


def _ple(x1, y, p2d, w_gate, w_proj, g_ple, g_final, *, d, tm):
    t = x1.shape[0]
    pd = p2d.shape[1]
    return pl.pallas_call(
        _ple_kernel,
        grid=(t // tm,),
        in_specs=[
            pl.BlockSpec((tm, d), lambda i: (i, 0)),
            pl.BlockSpec((tm, d), lambda i: (i, 0)),
            pl.BlockSpec((tm, pd), lambda i: (i, 0)),
            _resident((d, d), lambda i: (0, 0)),
            _resident((pd, d), lambda i: (0, 0)),
            pl.BlockSpec((1, d), lambda i: (0, 0)),
            pl.BlockSpec((1, d), lambda i: (0, 0)),
        ],
        out_specs=pl.BlockSpec((tm, d), lambda i: (i, 0)),
        out_shape=jax.ShapeDtypeStruct((t, d), F32),
        compiler_params=_params("parallel"),
        name="ple",
    )(x1, y, p2d, w_gate, w_proj, g_ple, g_final)


def _tiles(seq, d, d_ff):
    pick = lambda n, want: max(c for c in range(1, want + 1) if n % c == 0)
    return dict(
        in_tm=pick(seq, 1024), in_tn=pick(d, 1024), ret_rows=pick(seq, RET_ROWS_PER_STEP),
        mix_tm=pick(seq, 256), out_tm=pick(seq, 512),
        ffn_tm=pick(seq, 1024), ffn_tf=pick(d_ff, 512), ple_tm=pick(seq, 512),
    )


def kernel(x, p, g_mix, w_in, conv_w, w_conv_out, g_ret, w_ret_out, w_o, g_ffn, w_ffn_in,
           w_ffn_out, g_ple, w_ple_gate, w_ple_proj, g_final):
    bsz, seq, d = x.shape
    depth = w_in.shape[0]
    t = bsz * seq
    d_ff = w_ffn_out.shape[1]
    dk = d // RET_HEADS
    assert depth == 1, depth
    assert seq % RET_BLOCK == 0 and RET_BLOCK % CHUNK == 0
    ts = _tiles(seq, d, d_ff)
    assert ts["ret_rows"] % RET_BLOCK == 0
    cos, sin = _rotary_tables(seq, dk)
    bf = lambda w: w.astype(BF16)

    xc = x.reshape(t, d)
    pv, qk, act = _in_proj(xc, g_mix, bf(w_in[0]), cos, sin, seq=seq, d=d, tm=ts["in_tm"], tn=ts["in_tn"])
    a_ret = _retention(qk, pv, act, g_ret, bsz=bsz, seq=seq, d=d, rows=ts["ret_rows"])
    merged = _mix(pv, act, a_ret, conv_w[0], bf(w_conv_out[0]), bf(w_ret_out[0]),
                  seq=seq, d=d, tm=ts["mix_tm"])
    x1, h2 = _out_proj(merged, xc, bf(w_o[0]), g_ffn, d=d, tm=ts["out_tm"])
    y = _ffn(h2, bf(w_ffn_in[0]), bf(w_ffn_out[0]), d=d, tm=ts["ffn_tm"], tf=ts["ffn_tf"])
    out = _ple(x1, y, p[0].reshape(t, -1), bf(w_ple_gate[0]), bf(w_ple_proj[0]), g_ple, g_final[None],
               d=d, tm=ts["ple_tm"])
    return out.reshape(bsz, seq, d)
```

```python
import functools

import jax
import jax.numpy as jnp
from jax import lax
from jax.experimental import pallas as pl
from jax.experimental.pallas import tpu as pltpu

RET_HEADS = 8
CHUNK = 64
CONV_WIDTH = 3
ROPE_BASE = 10000.0
EPS = 1e-6

RET_BLOCK = 256
RET_ROWS_PER_STEP = 2048
PROJ_ROW_CHUNKS = 4
VMEM_LIMIT_BYTES = 56 * 1024 * 1024
BF16_SUBLANES = 16

F32 = jnp.float32
BF16 = jnp.bfloat16


def _params(*sem):
    return pltpu.CompilerParams(dimension_semantics=sem, vmem_limit_bytes=VMEM_LIMIT_BYTES)


def _resident(shape, index_map):
    return pl.BlockSpec(shape, index_map, pipeline_mode=pl.Buffered(1))


def _rms(xf, g):
    return xf * lax.rsqrt(jnp.mean(xf * xf, axis=-1, keepdims=True) + EPS) * g


def _dot(a, b):
    return jnp.dot(a, b, preferred_element_type=F32)


def _sigmoid(x):
    return 0.5 * jnp.tanh(0.5 * x) + 0.5


def _row_chunks(ref, n):
    rows = ref.shape[0] // n
    return tuple(pl.ds(c * rows, rows) for c in range(n))


def _log_gamma():
    return jnp.log1p(-jnp.exp2(-5.0 - jnp.arange(RET_HEADS, dtype=F32)))


def _norm_kernel(x_ref, g_ref, h_ref):
    h_ref[...] = _rms(x_ref[...], g_ref[...]).astype(BF16)


def _proj_plain_kernel(h_ref, w_ref, o_ref):
    w = w_ref[...].astype(BF16)
    for rows in _row_chunks(h_ref, PROJ_ROW_CHUNKS):
        o_ref[rows, :] = _dot(h_ref[rows, :], w).astype(BF16)


def _proj_rotary_kernel(lg_ref, h_ref, w_ref, cos_ref, sin_ref, o_ref, *, dk, k_blocks_from):
    j = pl.program_id(1)
    w = w_ref[...].astype(BF16)
    half = dk // 2
    heads_per_block = o_ref.shape[1] // dk
    is_k = j >= k_blocks_from
    head0 = (j - jnp.where(is_k, k_blocks_from, 0)) * heads_per_block
    for rows in _row_chunks(h_ref, PROJ_ROW_CHUNKS):
        acc = _dot(h_ref[rows, :], w)
        cos = cos_ref[0, rows, :]
        sin = sin_ref[0, rows, :]
        r = (lax.broadcasted_iota(jnp.int32, cos.shape, 0) + rows.start) % RET_BLOCK
        expo = jnp.where(is_k, RET_BLOCK - 1 - r, r + 1).astype(F32)
        for hh in range(heads_per_block):
            dec = jnp.exp(lg_ref[head0 + hh] * expo)
            c0 = hh * dk
            t1 = acc[:, c0:c0 + half]
            t2 = acc[:, c0 + half:c0 + dk]
            cd = cos * dec
            sd = sin * dec
            o_ref[rows, c0:c0 + half] = (t1 * cd - t2 * sd).astype(BF16)
            o_ref[rows, c0 + half:c0 + dk] = (t2 * cd + t1 * sd).astype(BF16)


def _proj_act_kernel(h_ref, w_ref, on_ref, o_ref, *, silu_blocks):
    w = w_ref[...].astype(BF16)
    is_silu = pl.program_id(1) < silu_blocks
    for rows in _row_chunks(h_ref, PROJ_ROW_CHUNKS):
        acc = _dot(h_ref[rows, :], w)
        gated = acc * on_ref[rows, :].astype(F32)
        o_ref[rows, :] = (_sigmoid(acc) * jnp.where(is_silu, gated, 1.0)).astype(BF16)


def _norm(x2d, g_mix, *, d, tm):
    t = x2d.shape[0]
    return pl.pallas_call(
        _norm_kernel,
        grid=(t // tm,),
        in_specs=[pl.BlockSpec((tm, d), lambda i: (i, 0)), pl.BlockSpec((1, d), lambda i: (0, 0))],
        out_specs=pl.BlockSpec((tm, d), lambda i: (i, 0)),
        out_shape=jax.ShapeDtypeStruct((t, d), BF16),
        compiler_params=_params("parallel"),
        name="norm",
    )(x2d, g_mix)


def _proj_rotary(h, w_in, cos, sin, *, seq, d, tm, tn):
    t = h.shape[0]
    dk = d // RET_HEADS
    nb = d // tn
    assert tm % RET_BLOCK == 0
    pos_blocks = seq // tm
    return pl.pallas_call(
        functools.partial(_proj_rotary_kernel, dk=dk, k_blocks_from=nb),
        grid=(t // tm, 2 * nb),
        in_specs=[
            pl.BlockSpec(memory_space=pltpu.SMEM),
            pl.BlockSpec((tm, d), lambda i, j: (i, 0)),
            pl.BlockSpec((d, tn), lambda i, j: (0, 3 * nb + j)),
            pl.BlockSpec((1, tm, dk // 2), lambda i, j: (j // nb, i % pos_blocks, 0)),
            pl.BlockSpec((1, tm, dk // 2), lambda i, j: (j // nb, i % pos_blocks, 0)),
        ],
        out_specs=pl.BlockSpec((tm, tn), lambda i, j: (i, j)),
        out_shape=jax.ShapeDtypeStruct((t, 2 * d), BF16),
        compiler_params=_params("parallel", "arbitrary"),
        name="proj_rotary",
    )(_log_gamma(), h, w_in, cos, sin)


def _proj_plain(h, w_in, *, d, tm, tn):
    t = h.shape[0]
    nb = d // tn
    return pl.pallas_call(
        _proj_plain_kernel,
        grid=(t // tm, 5 * nb),
        in_specs=[pl.BlockSpec((tm, d), lambda i, j: (i, 0)),
                  pl.BlockSpec((d, tn), lambda i, j: (0, jnp.where(j < 3 * nb, j, j + 2 * nb)))],
        out_specs=pl.BlockSpec((tm, tn), lambda i, j: (i, j)),
        out_shape=jax.ShapeDtypeStruct((t, 5 * d), BF16),
        compiler_params=_params("parallel", "arbitrary"),
        name="proj_plain",
    )(h, w_in)


def _proj_act(h, w_in, o_norm, *, d, tm, tn):
    t = h.shape[0]
    nb = d // tn
    return pl.pallas_call(
        functools.partial(_proj_act_kernel, silu_blocks=2 * nb),
        grid=(t // tm, 4 * nb),
        in_specs=[pl.BlockSpec((tm, d), lambda i, j: (i, 0)),
                  pl.BlockSpec((d, tn), lambda i, j: (0, 7 * nb + j)),
                  pl.BlockSpec((tm, tn), lambda i, j: (i, jnp.minimum(j, 2 * nb - 1)))],
        out_specs=pl.BlockSpec((tm, tn), lambda i, j: (i, j)),
        out_shape=jax.ShapeDtypeStruct((t, 4 * d), BF16),
        compiler_params=_params("parallel", "arbitrary"),
        name="proj_act",
    )(h, w_in, o_norm)


def _rotary_tables(seq, dk):
    half = dk // 2
    inv = ROPE_BASE ** (-jnp.arange(half, dtype=F32) / half)
    ang = jnp.arange(seq, dtype=jnp.int32).astype(F32)[:, None] * inv[None, :]
    scale = jnp.array([1.0, float(dk) ** -0.5], F32)[:, None, None]
    return jnp.cos(ang)[None] * scale, jnp.sin(ang)[None] * scale


def _retention_kernel(q_ref, k_ref, v_ref, dm_ref, gl_ref, gr_ref, o_ref, state_ref):
    @pl.when(pl.program_id(2) == 0)
    def _():
        state_ref[...] = jnp.zeros_like(state_ref)

    lb = dm_ref.shape[1]
    state = state_ref[...]
    for r0 in range(0, q_ref.shape[0], lb):
        rows = pl.ds(r0, lb)
        q = q_ref[rows, :]
        k = k_ref[rows, :]
        v = v_ref[rows, :]
        s = lax.dot_general(q, k, (((1,), (1,)), ((), ())), preferred_element_type=F32)
        p = (s * dm_ref[0]).astype(BF16)
        o = _dot(jnp.concatenate([p, q], axis=1), jnp.concatenate([v, state.astype(BF16)], axis=0))
        state = state * gl_ref[0] + lax.dot_general(
            k, v, (((0,), (0,)), ((), ())), preferred_element_type=F32)

        mu = jnp.mean(o, axis=-1, keepdims=True)
        oc = o - mu
        var = jnp.mean(oc * oc, axis=-1, keepdims=True)
        o_ref[rows, :] = (oc * lax.rsqrt(var + EPS) * gr_ref[...]).astype(BF16)
    state_ref[...] = state


def _retention_tables(dv):
    lg = _log_gamma()[:, None, None]
    r = jnp.arange(RET_BLOCK, dtype=F32)
    ahead = r[None, :] - r[:, None]
    chunk = jnp.arange(RET_BLOCK) // CHUNK
    same = chunk[:, None] == chunk[None, :]
    past = chunk[:, None] > chunk[None, :]
    expo = jnp.where(same & (ahead > 0), 2.0 * ahead, 0.0) - float(RET_BLOCK)
    dm = jnp.where(same | past, jnp.exp(lg * expo[None]), 0.0)
    gl = jnp.broadcast_to(jnp.exp(lg * float(RET_BLOCK)), (RET_HEADS, 1, dv))
    return dm, gl


def _retention(qk, pv, g_ret, *, bsz, seq, d, rows):
    t = qk.shape[0]
    dk = d // RET_HEADS
    dv = 2 * dk
    lb = RET_BLOCK
    ns = seq // rows
    dm, gl = _retention_tables(dv)
    row = lambda b, h, n: b * ns + n
    k0 = d // dk
    v0 = 3 * d // dv
    return pl.pallas_call(
        _retention_kernel,
        grid=(bsz, RET_HEADS, ns),
        in_specs=[
            pl.BlockSpec((rows, dk), lambda b, h, n: (row(b, h, n), h)),
            pl.BlockSpec((rows, dk), lambda b, h, n: (row(b, h, n), k0 + h)),
            pl.BlockSpec((rows, dv), lambda b, h, n: (row(b, h, n), v0 + h)),
            pl.BlockSpec((1, lb, lb), lambda b, h, n: (h, 0, 0)),
            pl.BlockSpec((1, 1, dv), lambda b, h, n: (h, 0, 0)),
            pl.BlockSpec((1, dv), lambda b, h, n: (0, h)),
        ],
        out_specs=pl.BlockSpec((rows, dv), lambda b, h, n: (row(b, h, n), h)),
        out_shape=jax.ShapeDtypeStruct((t, RET_HEADS * dv), BF16),
        scratch_shapes=[pltpu.VMEM((dk, dv), F32)],
        compiler_params=_params("parallel", "parallel", "arbitrary"),
        name="retention",
    )(qk, qk, pv, dm, gl, g_ret)


def _mix_kernel(cb_ref, cc_ref, cv_ref, hc_ref, hv_ref, sgc_ref, sgr_ref, ar_ref, cw_ref,
                wc_ref, wr_ref, o_ref, *, seq):
    tm = cb_ref.shape[0]
    halo = hc_ref.shape[0]
    u = cc_ref[...].astype(F32) * cv_ref[...].astype(F32)
    uh = hc_ref[...].astype(F32) * hv_ref[...].astype(F32)
    at_seq_start = (pl.program_id(0) * tm) % seq == 0
    uh = jnp.where(at_seq_start, 0.0, uh)
    ue = jnp.concatenate([uh, u], axis=0)
    u1 = pltpu.roll(ue, 1, 0)[halo:]
    u2 = pltpu.roll(ue, 2, 0)[halo:]
    cw = cw_ref[...]
    y = cw[0:1] * u2 + cw[1:2] * u1 + cw[2:3] * u
    a_conv = (cb_ref[...].astype(F32) * y).astype(BF16)
    y_ret = _dot(ar_ref[...], wr_ref[...])
    y_conv = _dot(a_conv, wc_ref[...])
    merged = sgc_ref[...].astype(F32) * y_conv + sgr_ref[...].astype(F32) * y_ret
    o_ref[...] = merged.astype(BF16)


def _mix(pv, act, conv_w, w_conv_out, w_ret_out, *, seq, d, tm):
    t = pv.shape[0]
    halo = BF16_SUBLANES
    hb = tm // halo
    prev = lambda i: jnp.maximum(i * hb - 1, 0)
    return pl.pallas_call(
        functools.partial(_mix_kernel, seq=seq),
        grid=(t // tm,),
        in_specs=[
            pl.BlockSpec((tm, d), lambda i: (i, 0)),
            pl.BlockSpec((tm, d), lambda i: (i, 1)),
            pl.BlockSpec((tm, d), lambda i: (i, 2)),
            pl.BlockSpec((halo, d), lambda i: (prev(i), 1)),
            pl.BlockSpec((halo, d), lambda i: (prev(i), 2)),
            pl.BlockSpec((tm, d), lambda i: (i, 2)),
            pl.BlockSpec((tm, d), lambda i: (i, 3)),
            pl.BlockSpec((tm, 2 * d), lambda i: (i, 0)),
            pl.BlockSpec((CONV_WIDTH, d), lambda i: (0, 0)),
            _resident((d, d), lambda i: (0, 0)),
            _resident((2 * d, d), lambda i: (0, 0)),
        ],
        out_specs=pl.BlockSpec((tm, d), lambda i: (i, 0)),
        out_shape=jax.ShapeDtypeStruct((t, d), BF16),
        compiler_params=_params("parallel"),
        name="mix",
    )(pv, pv, pv, pv, pv, act, act, act, conv_w, w_conv_out, w_ret_out)


def _out_proj_kernel(m_ref, x_ref, w_ref, g_ref, x1_ref, h_ref):
    for rows in _row_chunks(m_ref, 2):
        x1 = x_ref[rows, :] + _dot(m_ref[rows, :], w_ref[...])
        x1_ref[rows, :] = x1
        h_ref[rows, :] = _rms(x1, g_ref[...]).astype(BF16)


def _out_proj(merged, x2d, w_o, g_ffn, *, d, tm):
    t = x2d.shape[0]
    return pl.pallas_call(
        _out_proj_kernel,
        grid=(t // tm,),
        in_specs=[
            pl.BlockSpec((tm, d), lambda i: (i, 0)),
            pl.BlockSpec((tm, d), lambda i: (i, 0)),
            _resident((d, d), lambda i: (0, 0)),
            pl.BlockSpec((1, d), lambda i: (0, 0)),
        ],
        out_specs=[pl.BlockSpec((tm, d), lambda i: (i, 0)), pl.BlockSpec((tm, d), lambda i: (i, 0))],
        out_shape=[jax.ShapeDtypeStruct((t, d), F32), jax.ShapeDtypeStruct((t, d), BF16)],
        compiler_params=_params("parallel"),
        name="out_proj",
    )(merged, x2d, w_o, g_ffn)


def _ffn_kernel(h_ref, wa_ref, wb_ref, wo_ref, y_ref):
    @pl.when(pl.program_id(1) == 0)
    def _():
        y_ref[...] = jnp.zeros_like(y_ref)

    h = h_ref[...]
    a = _dot(h, wa_ref[...])
    b = _dot(h, wb_ref[...])
    act = (a * _sigmoid(a) * b).astype(BF16)
    y_ref[...] += _dot(act, wo_ref[...])


def _ffn(h2, w_ffn_in, w_ffn_out, *, d, tm, tf):
    t = h2.shape[0]
    d_ff = w_ffn_out.shape[0]
    nf = d_ff // tf
    return pl.pallas_call(
        _ffn_kernel,
        grid=(t // tm, nf),
        in_specs=[
            pl.BlockSpec((tm, d), lambda i, f: (i, 0)),
            pl.BlockSpec((d, tf), lambda i, f: (0, f)),
            pl.BlockSpec((d, tf), lambda i, f: (0, nf + f)),
            pl.BlockSpec((tf, d), lambda i, f: (f, 0)),
        ],
        out_specs=pl.BlockSpec((tm, d), lambda i, f: (i, 0)),
        out_shape=jax.ShapeDtypeStruct((t, d), F32),
        compiler_params=_params("parallel", "arbitrary"),
        name="ffn",
    )(h2, w_ffn_in, w_ffn_in, w_ffn_out)


def _ple_kernel(x_ref, y_ref, p_ref, wg_ref, wp_ref, gp_ref, gf_ref, o_ref):
    for rows in _row_chunks(x_ref, 2):
        x2 = x_ref[rows, :] + y_ref[rows, :]
        h3 = _rms(x2, gp_ref[...]).astype(BF16)
        gate = _sigmoid(_dot(h3, wg_ref[...]))
        pp = _dot(p_ref[rows, :].astype(BF16), wp_ref[...])
        x3 = x2 + gate * pp
        o_ref[rows, :] = _rms(x3, gf_ref[...])


def _ple(x1, y, p2d, w_gate, w_proj, g_ple, g_final, *, d, tm):
    t = x1.shape[0]
    pd = p2d.shape[1]
    return pl.pallas_call(
        _ple_kernel,
        grid=(t // tm,),
        in_specs=[
            pl.BlockSpec((tm, d), lambda i: (i, 0)),
            pl.BlockSpec((tm, d), lambda i: (i, 0)),
            pl.BlockSpec((tm, pd), lambda i: (i, 0)),
            _resident((d, d), lambda i: (0, 0)),
            _resident((pd, d), lambda i: (0, 0)),
            pl.BlockSpec((1, d), lambda i: (0, 0)),
            pl.BlockSpec((1, d), lambda i: (0, 0)),
        ],
        out_specs=pl.BlockSpec((tm, d), lambda i: (i, 0)),
        out_shape=jax.ShapeDtypeStruct((t, d), F32),
        compiler_params=_params("parallel"),
        name="ple",
    )(x1, y, p2d, w_gate, w_proj, g_ple, g_final)


def _tiles(seq, d, d_ff):
    pick = lambda n, want: max(c for c in range(1, want + 1) if n % c == 0)
    return dict(
        norm_tm=pick(seq, 512), in_tm=pick(seq, 2048), in_tn=pick(d, 1024),
        ret_rows=pick(seq, RET_ROWS_PER_STEP),
        mix_tm=pick(seq, 256), out_tm=pick(seq, 512),
        ffn_tm=pick(seq, 1024), ffn_tf=pick(d_ff, 512), ple_tm=pick(seq, 512),
    )


def kernel(x, p, g_mix, w_in, conv_w, w_conv_out, g_ret, w_ret_out, w_o, g_ffn, w_ffn_in,
           w_ffn_out, g_ple, w_ple_gate, w_ple_proj, g_final):
    bsz, seq, d = x.shape
    depth = w_in.shape[0]
    t = bsz * seq
    d_ff = w_ffn_out.shape[1]
    dk = d // RET_HEADS
    assert depth == 1, depth
    assert seq % RET_BLOCK == 0 and RET_BLOCK % CHUNK == 0
    ts = _tiles(seq, d, d_ff)
    assert ts["ret_rows"] % RET_BLOCK == 0
    in_tiles = dict(d=d, tm=ts["in_tm"], tn=ts["in_tn"])
    cos, sin = _rotary_tables(seq, dk)
    bf = lambda w: w.astype(BF16)

    xc = x.reshape(t, d)
    h = _norm(xc, g_mix, d=d, tm=ts["norm_tm"])
    qk = _proj_rotary(h, w_in[0], cos, sin, seq=seq, **in_tiles)
    pv = _proj_plain(h, w_in[0], **in_tiles)
    o_norm = _retention(qk, pv, g_ret, bsz=bsz, seq=seq, d=d, rows=ts["ret_rows"])
    act = _proj_act(h, w_in[0], o_norm, **in_tiles)
    merged = _mix(pv, act, conv_w[0], bf(w_conv_out[0]), bf(w_ret_out[0]), seq=seq, d=d, tm=ts["mix_tm"])
    x1, h2 = _out_proj(merged, xc, bf(w_o[0]), g_ffn, d=d, tm=ts["out_tm"])
    y = _ffn(h2, bf(w_ffn_in[0]), bf(w_ffn_out[0]), d=d, tm=ts["ffn_tm"], tf=ts["ffn_tf"])
    out = _ple(x1, y, p[0].reshape(t, -1), bf(w_ple_gate[0]), bf(w_ple_proj[0]), g_ple, g_final[None],
               d=d, tm=ts["ple_tm"])
    return out.reshape(bsz, seq, d)
```
